```python
import math
import jax, jax.numpy as jnp
from jax import lax
import numpy as np

D_MODEL = 4096
BATCH = 2
SEQ = 4096
DEPTH = 1

MIX_WIDTH = D_MODEL
GDN_HEADS = 16
GDN_HEAD_DIM = 128
GDN_WIDTH = GDN_HEADS * GDN_HEAD_DIM
GDN_QKV = 3 * GDN_WIDTH
CONV_WIDTH = 4
CHUNK = 64
DIFF_HEADS = 8
DIFF_HEAD_DIM = 128
DIFF_V_DIM = 2 * DIFF_HEAD_DIM
DIFF_WIDTH = DIFF_HEADS * DIFF_V_DIM
ROT_DIM = DIFF_HEAD_DIM // 4
ROPE_THETA = 500000.0
Q_BLOCK = 128
PEER_HEADS = 8
N_KEYS = 128
N_EXPERTS = N_KEYS * N_KEYS
PEER_KEY_DIM = 128
PEER_QUERY_DIM = 2 * PEER_KEY_DIM
PEER_TOPK = 16
PEER_TOKEN_BLOCK = 64
EPS = 1e-6

IN_SPLITS = (GDN_QKV, GDN_QKV + GDN_WIDTH, GDN_QKV + GDN_WIDTH + GDN_HEADS,
             GDN_QKV + GDN_WIDTH + 2 * GDN_HEADS,
             GDN_QKV + GDN_WIDTH + 2 * GDN_HEADS + DIFF_WIDTH,
             GDN_QKV + GDN_WIDTH + 2 * GDN_HEADS + 2 * DIFF_WIDTH)
IN_COLS = GDN_QKV + GDN_WIDTH + 2 * GDN_HEADS + 3 * DIFF_WIDTH

kernel_name = "hybrid_gdn_diffattn_peer"


def rms_norm(x, w):
    x32 = x.astype(jnp.float32)
    y = x32 * lax.rsqrt(jnp.mean(x32 * x32, axis=-1, keepdims=True) + EPS)
    return (y * w.astype(jnp.float32)).astype(x.dtype)


def l2norm(t):
    return t * lax.rsqrt(jnp.sum(t * t, axis=-1, keepdims=True) + EPS)


def causal_conv_silu(x, w):
    c = x.shape[-1]
    y = lax.conv_general_dilated(x, w[:, None, :].astype(x.dtype), window_strides=(1,),
                                 padding=[(CONV_WIDTH - 1, 0)],
                                 dimension_numbers=('NWC', 'WIO', 'NWC'),
                                 feature_group_count=c)
    return jax.nn.silu(y)


def gated_delta_rule(q, k, v, g, beta):
    b, h, s, dk = q.shape
    dv = v.shape[-1]
    n = s // CHUNK
    q = l2norm(q) * (dk ** -0.5)
    k = l2norm(k)
    resh = lambda t: t.reshape((b, h, n, CHUNK) + t.shape[3:])
    q, k, v, g, beta = resh(q), resh(k), resh(v), resh(g), resh(beta)
    g = jnp.cumsum(g, axis=-1)
    tril = jnp.tril(jnp.ones((CHUNK, CHUNK), dtype=bool))
    strict = jnp.tril(jnp.ones((CHUNK, CHUNK), dtype=bool), -1)
    decay = jnp.exp(jnp.where(tril, g[..., :, None] - g[..., None, :], -jnp.inf))
    k_beta = k * beta[..., None]
    v_beta = v * beta[..., None]
    lmat = jnp.where(strict, jnp.einsum('bhncd,bhnkd->bhnck', k_beta, k) * decay, 0.0)
    a = lmat + jnp.eye(CHUNK, dtype=jnp.float32)
    rhs = jnp.concatenate([v_beta, k_beta * jnp.exp(g)[..., None]], axis=-1)
    sol = lax.linalg.triangular_solve(a, rhs, left_side=True, lower=True, unit_diagonal=True)
    u, w = sol[..., :dv], sol[..., dv:]
    attn_intra = jnp.einsum('bhncd,bhnkd->bhnck', q, k) * decay
    q_dec = q * jnp.exp(g)[..., None]
    g_last = g[..., -1]
    k_dec = k * jnp.exp(g_last[..., None] - g)[..., None]

    def step(state, inp):
        u_i, w_i, a_i, qd_i, kd_i, gl_i = inp
        v_new = u_i - jnp.einsum('bhck,bhkv->bhcv', w_i, state)
        o = (jnp.einsum('bhck,bhkv->bhcv', qd_i, state)
             + jnp.einsum('bhcj,bhjv->bhcv', a_i, v_new))
        state = (state * jnp.exp(gl_i)[..., None, None]
                 + jnp.einsum('bhck,bhcv->bhkv', kd_i, v_new))
        return state, o

    xs = (jnp.moveaxis(u, 2, 0), jnp.moveaxis(w, 2, 0), jnp.moveaxis(attn_intra, 2, 0),
          jnp.moveaxis(q_dec, 2, 0), jnp.moveaxis(k_dec, 2, 0), jnp.moveaxis(g_last, 2, 0))
    state0 = jnp.zeros((b, h, dk, dv), jnp.float32)
    _, o = lax.scan(step, state0, xs)
    return jnp.moveaxis(o, 0, 2).reshape(b, h, s, dv)


def partial_rope(t, cos, sin):
    t32 = t.astype(jnp.float32)
    half = ROT_DIM // 2
    x1, x2, rest = t32[..., :half], t32[..., half:ROT_DIM], t32[..., ROT_DIM:]
    r1 = x1 * cos - x2 * sin
    r2 = x2 * cos + x1 * sin
    return jnp.concatenate([r1, r2, rest], axis=-1).astype(t.dtype)


def diff_attention(q, k, v, lam):
    b, s, h, _, d = q.shape
    scale = d ** -0.5
    nb = s // Q_BLOCK
    qb = jnp.moveaxis(q.reshape(b, nb, Q_BLOCK, h, 2, d), 1, 0)
    kpos = jnp.arange(s)

    def block(args):
        qi, bi = args
        scores = jnp.einsum('bqhcd,bkhcd->bhcqk', qi, k).astype(jnp.float32) * scale
        qpos = bi * Q_BLOCK + jnp.arange(Q_BLOCK)
        mask = kpos[None, :] <= qpos[:, None]
        p = jax.nn.softmax(jnp.where(mask, scores, -jnp.inf), axis=-1)
        attn = p[:, :, 0] - lam * p[:, :, 1]
        return jnp.einsum('bhqk,bkhv->bqhv', attn.astype(v.dtype), v)

    out = lax.map(block, (qb, jnp.arange(nb)))
    return jnp.moveaxis(out, 0, 1).reshape(b, s, h, 2 * d)


def peer(x, w_query, keys1, keys2, expert_down, expert_up):
    b, s, dm = x.shape
    t = x.reshape(b * s, dm)
    ntok = b * s
    q = (t @ w_query).reshape(ntok, PEER_HEADS, 2, PEER_KEY_DIM)
    s1 = jnp.einsum('thd,hnd->thn', q[:, :, 0], keys1).astype(jnp.float32)
    s2 = jnp.einsum('thd,hnd->thn', q[:, :, 1], keys2).astype(jnp.float32)
    v1, i1 = lax.top_k(s1, PEER_TOPK)
    v2, i2 = lax.top_k(s2, PEER_TOPK)
    cand = (v1[..., :, None] + v2[..., None, :]).reshape(ntok, PEER_HEADS, PEER_TOPK * PEER_TOPK)
    cand_idx = (i1[..., :, None] * N_KEYS + i2[..., None, :]).reshape(ntok, PEER_HEADS, PEER_TOPK * PEER_TOPK)
    top_s, pos = lax.top_k(cand, PEER_TOPK)
    idx = jnp.take_along_axis(cand_idx, pos, axis=-1)
    gate = jax.nn.softmax(top_s, axis=-1)
    nblk = ntok // PEER_TOKEN_BLOCK

    def block(args):
        tb, ib, gb = args
        u = expert_down[ib]
        hid = jax.nn.gelu(jnp.einsum('td,thkd->thk', tb, u).astype(jnp.float32), approximate=False)
        wgt = (gb * hid).astype(tb.dtype)
        return jnp.einsum('thk,thkd->td', wgt, expert_up[ib])

    out = lax.map(block, (t.reshape(nblk, PEER_TOKEN_BLOCK, dm),
                          idx.reshape(nblk, PEER_TOKEN_BLOCK, PEER_HEADS, PEER_TOPK),
                          gate.reshape(nblk, PEER_TOKEN_BLOCK, PEER_HEADS, PEER_TOPK)))
    return out.reshape(b, s, dm)


def setup_inputs(seed: int = 0) -> dict:
    key = jax.random.key(seed)
    ks = jax.random.split(key, 24)
    L = DEPTH
    nrm = jax.random.normal
    dt = jnp.exp(jax.random.uniform(ks[4], (L, GDN_HEADS), minval=math.log(1e-3), maxval=math.log(1e-1)))
    return {
        "x": nrm(ks[0], (BATCH, SEQ, D_MODEL), jnp.float32),
        "norm_mix_w": 1.0 + 0.02 * nrm(ks[1], (L, D_MODEL), jnp.float32),
        "w_in": nrm(ks[2], (L, D_MODEL, IN_COLS), jnp.float32) * D_MODEL ** -0.5,
        "gdn_conv": nrm(ks[3], (L, CONV_WIDTH, GDN_QKV), jnp.float32) * CONV_WIDTH ** -0.5,
        "gdn_a_log": jnp.log(jax.random.uniform(ks[5], (L, GDN_HEADS), minval=1.0, maxval=16.0)),
        "gdn_dt_bias": dt + jnp.log(-jnp.expm1(-dt)),
        "gdn_norm_w": 1.0 + 0.02 * nrm(ks[6], (L, GDN_HEAD_DIM), jnp.float32),
        "lambda_q1": 0.1 * nrm(ks[7], (L, DIFF_HEAD_DIM), jnp.float32),
        "lambda_k1": 0.1 * nrm(ks[8], (L, DIFF_HEAD_DIM), jnp.float32),
        "lambda_q2": 0.1 * nrm(ks[9], (L, DIFF_HEAD_DIM), jnp.float32),
        "lambda_k2": 0.1 * nrm(ks[10], (L, DIFF_HEAD_DIM), jnp.float32),
        "diff_norm_w": 1.0 + 0.02 * nrm(ks[11], (L, DIFF_V_DIM), jnp.float32),
        "w_out": nrm(ks[12], (L, MIX_WIDTH, D_MODEL), jnp.float32) * MIX_WIDTH ** -0.5,
        "norm_ffn_w": 1.0 + 0.02 * nrm(ks[13], (L, D_MODEL), jnp.float32),
        "peer_w_query": nrm(ks[14], (L, D_MODEL, PEER_HEADS * PEER_QUERY_DIM), jnp.float32) * D_MODEL ** -0.5,
        "peer_keys1": nrm(ks[15], (L, PEER_HEADS, N_KEYS, PEER_KEY_DIM), jnp.float32) * PEER_KEY_DIM ** -0.5,
        "peer_keys2": nrm(ks[16], (L, PEER_HEADS, N_KEYS, PEER_KEY_DIM), jnp.float32) * PEER_KEY_DIM ** -0.5,
        "peer_down": nrm(ks[17], (L, N_EXPERTS, D_MODEL), jnp.float32) * D_MODEL ** -0.5,
        "peer_up": nrm(ks[18], (L, N_EXPERTS, D_MODEL), jnp.float32) * 0.5,
        "norm_final_w": 1.0 + 0.02 * nrm(ks[19], (D_MODEL,), jnp.float32),
    }


def reference(x, norm_mix_w, w_in, gdn_conv, gdn_a_log, gdn_dt_bias, gdn_norm_w,
              lambda_q1, lambda_k1, lambda_q2, lambda_k2, diff_norm_w, w_out,
              norm_ffn_w, peer_w_query, peer_keys1, peer_keys2, peer_down, peer_up,
              norm_final_w):
    b, s, _ = x.shape
    f32 = jnp.float32
    pos = jnp.arange(s, dtype=f32)
    inv_freq = jnp.float32(ROPE_THETA) ** (-jnp.arange(0, ROT_DIM, 2, dtype=f32) / ROT_DIM)
    ang = pos[:, None] * inv_freq[None, :]
    cos = jnp.cos(ang)[:, None, None, :]
    sin = jnp.sin(ang)[:, None, None, :]

    for l in range(DEPTH):
        h = rms_norm(x, norm_mix_w[l])
        proj = h @ w_in[l]
        qkv_g, z, b_logit, a_in, dq, dk, dv = jnp.split(proj, IN_SPLITS, axis=-1)

        qkv_g = causal_conv_silu(qkv_g, gdn_conv[l])
        gq, gk, gv = jnp.split(qkv_g, 3, axis=-1)
        to_heads = lambda t: t.reshape(b, s, GDN_HEADS, GDN_HEAD_DIM).transpose(0, 2, 1, 3).astype(f32)
        beta = jax.nn.sigmoid(b_logit.astype(f32)).transpose(0, 2, 1)
        g = (-jnp.exp(gdn_a_log[l].astype(f32))
             * jax.nn.softplus(a_in.astype(f32) + gdn_dt_bias[l].astype(f32))).transpose(0, 2, 1)
        o_g = gated_delta_rule(to_heads(gq), to_heads(gk), to_heads(gv), g, beta)
        o_g = o_g.transpose(0, 2, 1, 3)
        zh = z.reshape(b, s, GDN_HEADS, GDN_HEAD_DIM).astype(f32)
        o_g = (rms_norm(o_g, gdn_norm_w[l]) * jax.nn.silu(zh)).astype(x.dtype).reshape(b, s, GDN_WIDTH)

        dq = partial_rope(dq.reshape(b, s, DIFF_HEADS, 2, DIFF_HEAD_DIM), cos, sin)
        dk = partial_rope(dk.reshape(b, s, DIFF_HEADS, 2, DIFF_HEAD_DIM), cos, sin)
        dv = dv.reshape(b, s, DIFF_HEADS, DIFF_V_DIM)
        lam_init = 0.8 - 0.6 * math.exp(-0.3 * l)
        lam = (jnp.exp(jnp.sum(lambda_q1[l].astype(f32) * lambda_k1[l].astype(f32)))
               - jnp.exp(jnp.sum(lambda_q2[l].astype(f32) * lambda_k2[l].astype(f32)))
               + lam_init)
        o_d = diff_attention(dq, dk, dv, lam)
        o_d = (rms_norm(o_d, diff_norm_w[l]).astype(f32) * (1.0 - lam_init)).astype(x.dtype)
        o_d = o_d.reshape(b, s, DIFF_WIDTH)

        x = x + jnp.concatenate([o_g, o_d], axis=-1) @ w_out[l]

        hf = rms_norm(x, norm_ffn_w[l])
        x = x + peer(hf, peer_w_query[l], peer_keys1[l], peer_keys2[l], peer_down[l], peer_up[l]).astype(x.dtype)

    return rms_norm(x, norm_final_w)
```

```python
import functools
import math

import jax
import jax.numpy as jnp
from jax import lax
from jax.experimental import pallas as pl
from jax.experimental.pallas import tpu as pltpu

F32 = jnp.float32
BF16 = jnp.bfloat16
I32 = jnp.int32
EPS = 1e-6

GDN_HEADS = 16
HEAD_DIM = 128
GDN_WIDTH = GDN_HEADS * HEAD_DIM
CONV_WIDTH = 4
CHUNK = 64
DIFF_HEADS = 8
DIFF_V_DIM = 2 * HEAD_DIM
DIFF_WIDTH = DIFF_HEADS * DIFF_V_DIM
ROT_DIM = HEAD_DIM // 4
ROPE_THETA = 500000.0
PEER_HEADS = 8
N_KEYS = 128
PEER_TOPK = 16
LAMBDA_INIT = 0.8 - 0.6 * math.exp(-0.3 * 0)

VMEM_LIMIT_BYTES = 56 * 1024 * 1024

NT_DIMS = (((1,), (1,)), ((), ()))
TN_DIMS = (((0,), (0,)), ((), ()))


def _params(*semantics):
    return pltpu.CompilerParams(dimension_semantics=semantics, vmem_limit_bytes=VMEM_LIMIT_BYTES)


def _dot(a, b, dims=None, precision=None):
    if dims is None:
        return jnp.dot(a, b, preferred_element_type=F32, precision=precision)
    return lax.dot_general(a, b, dims, preferred_element_type=F32, precision=precision)


def _sigmoid(x):
    return 1.0 / (1.0 + jnp.exp(-x))


def _rms(x):
    return x * lax.rsqrt(jnp.mean(x * x, axis=-1, keepdims=True) + EPS)


def _norm_matmul_body(x_ref, nw_ref, w_ref, o_ref, *rest, emit_h):
    h_ref = rest[-1]

    @pl.when(pl.program_id(1) == 0)
    def _():
        h_ref[...] = (_rms(x_ref[...]) * nw_ref[...]).astype(BF16)
        if emit_h:
            rest[0][...] = h_ref[...]

    o_ref[...] = _dot(h_ref[...], w_ref[...])


def norm_matmul(x, nw, w, *, tm, tn, emit_h=False):
    t, d = x.shape
    n = w.shape[1]
    out_shape = [jax.ShapeDtypeStruct((t, n), F32)]
    out_specs = [pl.BlockSpec((tm, tn), lambda i, j: (i, j))]
    if emit_h:
        out_shape.append(jax.ShapeDtypeStruct((t, d), BF16))
        out_specs.append(pl.BlockSpec((tm, d), lambda i, j: (i, 0)))
    res = pl.pallas_call(
        functools.partial(_norm_matmul_body, emit_h=emit_h),
        grid=(t // tm, n // tn),
        in_specs=[pl.BlockSpec((tm, d), lambda i, j: (i, 0)),
                  pl.BlockSpec((1, d), lambda i, j: (0, 0)),
                  pl.BlockSpec((d, tn), lambda i, j: (0, j))],
        out_specs=out_specs,
        out_shape=out_shape,
        scratch_shapes=[pltpu.VMEM((tm, d), BF16)],
        compiler_params=_params("parallel", "arbitrary"),
        name="norm_matmul",
    )(x, nw.reshape(1, d), w)
    return res if emit_h else res[0]


def _inv_unit_lower(lmat, eye):
    hi = lax.Precision.HIGHEST
    x = eye - lmat
    p = lmat
    for _ in range(int(math.log2(CHUNK)) - 1):
        p = _dot(p, p, precision=hi)
        x = x + _dot(x, p, precision=hi)
    return x


def _gdn_body(q_ref, k_ref, v_ref, z_ref, bl_ref, al_ref, cq_ref, ck_ref, cv_ref, alog_ref, dtb_ref,
              nw_ref, o_ref, st_ref, xq_ref, xk_ref, xv_ref, *, rows):
    hi = lax.Precision.HIGHEST
    head = pl.program_id(1)

    @pl.when(pl.program_id(2) == 0)
    def _():
        st_ref[...] = jnp.zeros_like(st_ref)
        for xs in (xq_ref, xk_ref, xv_ref):
            xs[0:8, :] = jnp.zeros((8, HEAD_DIM), F32)

    def conv_silu(x_ref, xs_ref, c_ref):
        x = x_ref[...]
        xs_ref[8:8 + rows, :] = x
        w = c_ref[...]
        y = x * w[3:4, :]
        for j in range(CONV_WIDTH - 1):
            y = y + xs_ref[5 + j:5 + j + rows, :] * w[j:j + 1, :]
        xs_ref[0:8, :] = x[rows - 8:rows, :]
        return y * _sigmoid(y)

    def l2norm(t):
        return t * lax.rsqrt(jnp.sum(t * t, axis=-1, keepdims=True) + EPS)

    q = l2norm(conv_silu(q_ref, xq_ref, cq_ref)) * (HEAD_DIM ** -0.5)
    k = l2norm(conv_silu(k_ref, xk_ref, ck_ref))
    v = conv_silu(v_ref, xv_ref, cv_ref)

    lane = lax.broadcasted_iota(I32, (rows, HEAD_DIM), 1)
    sel = lane == head
    a_all = al_ref[...] + dtb_ref[...]
    softplus = jnp.maximum(a_all, 0.0) + jnp.log1p(jnp.exp(-jnp.abs(a_all)))
    g_all = -jnp.exp(alog_ref[...]) * softplus
    g_col = jnp.sum(jnp.where(sel, g_all, 0.0), axis=1, keepdims=True)
    beta = jnp.sum(jnp.where(sel, _sigmoid(bl_ref[...]), 0.0), axis=1, keepdims=True)
    g_b = jnp.broadcast_to(g_col, (rows, HEAD_DIM))
    ri = lax.broadcasted_iota(I32, (rows, rows), 0)
    ci = lax.broadcasted_iota(I32, (rows, rows), 1)
    blocktril = jnp.where((ri >= ci) & (ri // CHUNK == ci // CHUNK), 1.0, 0.0).astype(F32)
    gcum = _dot(blocktril, g_b, precision=hi)
    e8 = jnp.where(lax.broadcasted_iota(I32, (8, HEAD_DIM), 0) == lax.broadcasted_iota(I32, (8, HEAD_DIM), 1),
                   1.0, 0.0).astype(F32)
    grow = _dot(e8, gcum, NT_DIMS, precision=hi)

    cr = lax.broadcasted_iota(I32, (CHUNK, CHUNK), 0)
    cc = lax.broadcasted_iota(I32, (CHUNK, CHUNK), 1)
    tril = cr >= cc
    strict = cr > cc
    eye = jnp.where(cr == cc, 1.0, 0.0).astype(F32)
    nw = nw_ref[...]

    for n in range(rows // CHUNK):
        sl = slice(n * CHUNK, (n + 1) * CHUNK)
        qn, kn, vn = q[sl], k[sl], v[sl]
        bn = beta[sl]
        gc = gcum[sl]
        gk = grow[0:1, sl]
        kb = kn * bn
        vb = vn * bn
        s1 = _dot(jnp.concatenate([kb, qn], axis=0).astype(BF16), kn.astype(BF16), NT_DIMS)
        dec = jnp.where(tril, jnp.exp(jnp.where(tril, gc[:, :CHUNK] - gk, 0.0)), 0.0)
        lmat = jnp.where(strict, s1[:CHUNK] * dec, 0.0)
        tinv = _inv_unit_lower(lmat, eye)
        eg = jnp.exp(gc)
        sol = _dot(tinv, jnp.concatenate([vb, kb * eg], axis=1), precision=hi)
        u, w = sol[:, :HEAD_DIM], sol[:, HEAD_DIM:]
        a_intra = s1[CHUNK:] * dec
        qd = qn * eg
        glast = gc[CHUNK - 1:CHUNK, :]
        kd = kn * jnp.exp(glast - gc)
        state = st_ref[...]
        ws = _dot(jnp.concatenate([w, qd], axis=0).astype(BF16), state.astype(BF16))
        v_new = u - ws[:CHUNK]
        vnb = v_new.astype(BF16)
        o = ws[CHUNK:] + _dot(a_intra.astype(BF16), vnb)
        st_ref[...] = state * jnp.exp(glast) + _dot(kd.astype(BF16), vnb, TN_DIMS)
        zn = z_ref[sl, :]
        o_ref[sl, :] = (_rms(o) * nw * (zn * _sigmoid(zn))).astype(o_ref.dtype)


def gated_deltanet(proj, ba, conv, a_log, dt_bias, norm_w, *, batch, seq, rows):
    t = batch * seq
    nr = seq // rows

    def col(base):
        return pl.BlockSpec((rows, HEAD_DIM), lambda b, h, r, base=base: (b * nr + r, base + h))

    def cw(base):
        return pl.BlockSpec((CONV_WIDTH, HEAD_DIM), lambda b, h, r, base=base: (0, base + h))

    def row128():
        return pl.BlockSpec((1, HEAD_DIM), lambda b, h, r: (0, 0))

    pad = lambda p: jnp.pad(p.astype(F32), (0, HEAD_DIM - p.shape[0])).reshape(1, HEAD_DIM)
    return pl.pallas_call(
        functools.partial(_gdn_body, rows=rows),
        grid=(batch, GDN_HEADS, nr),
        in_specs=[col(0), col(GDN_HEADS), col(2 * GDN_HEADS), col(3 * GDN_HEADS),
                  pl.BlockSpec((rows, HEAD_DIM), lambda b, h, r: (b * nr + r, 0)),
                  pl.BlockSpec((rows, HEAD_DIM), lambda b, h, r: (b * nr + r, 1)),
                  cw(0), cw(GDN_HEADS), cw(2 * GDN_HEADS), row128(), row128(), row128()],
        out_specs=pl.BlockSpec((rows, HEAD_DIM), lambda b, h, r: (b * nr + r, h)),
        out_shape=jax.ShapeDtypeStruct((t, GDN_WIDTH), BF16),
        scratch_shapes=[pltpu.VMEM((HEAD_DIM, HEAD_DIM), F32)] + [pltpu.VMEM((rows + 8, HEAD_DIM), F32)] * 3,
        compiler_params=_params("parallel", "parallel", "arbitrary"),
        name="gated_deltanet",
    )(proj, proj, proj, proj, ba, ba, conv, conv, conv, pad(a_log), pad(dt_bias), norm_w.reshape(1, HEAD_DIM))


def _rope_body(x_ref, c_ref, s1_ref, s2_ref, o_ref):
    x = x_ref[...]
    w = x.shape[-1]
    half = ROT_DIM // 2
    y = x * c_ref[...] + pltpu.roll(x, w - half, 1) * s1_ref[...] + pltpu.roll(x, half, 1) * s2_ref[...]
    o_ref[...] = y.astype(o_ref.dtype)


def rope_tables(seq):
    half = ROT_DIM // 2
    pos = jnp.arange(seq, dtype=F32)
    inv_freq = jnp.float32(ROPE_THETA) ** (-jnp.arange(0, ROT_DIM, 2, dtype=F32) / ROT_DIM)
    ang = pos[:, None] * inv_freq[None, :]
    cos, sin = jnp.cos(ang), jnp.sin(ang)
    zeros = jnp.zeros((seq, HEAD_DIM - ROT_DIM), F32)
    c = jnp.concatenate([cos, cos, jnp.ones_like(zeros)], axis=1)
    s1 = jnp.concatenate([-sin, jnp.zeros((seq, half), F32), zeros], axis=1)
    s2 = jnp.concatenate([jnp.zeros((seq, half), F32), sin, zeros], axis=1)
    return tuple(jnp.concatenate([t, t], axis=1) for t in (c, s1, s2))


def rope(proj, col0, tables, *, seq, tr):
    t = proj.shape[0]
    ns = seq // tr
    cb = col0 // DIFF_V_DIM
    tab = pl.BlockSpec((tr, DIFF_V_DIM), lambda i, j: (i % ns, 0))
    return pl.pallas_call(
        _rope_body,
        grid=(t // tr, DIFF_HEADS),
        in_specs=[pl.BlockSpec((tr, DIFF_V_DIM), lambda i, j: (i, cb + j)), tab, tab, tab],
        out_specs=pl.BlockSpec((tr, DIFF_V_DIM), lambda i, j: (i, j)),
        out_shape=jax.ShapeDtypeStruct((t, DIFF_WIDTH), BF16),
        compiler_params=_params("parallel", "parallel"),
        name="rope",
    )(proj, *tables)


def _attn_body(q_ref, k_ref, v_ref, lq1_ref, lk1_ref, lq2_ref, lk2_ref, nw_ref, o_ref, m_ref, l_ref, acc_ref,
               *, tq, tk, nk):
    qi = pl.program_id(2)
    ki = pl.program_id(3)

    @pl.when(ki == 0)
    def _():
        m_ref[...] = jnp.full_like(m_ref, -jnp.inf)
        l_ref[...] = jnp.zeros_like(l_ref)
        acc_ref[...] = jnp.zeros_like(acc_ref)

    @pl.when(ki * tk <= qi * tq + (tq - 1))
    def _():
        q = q_ref[...]
        k = k_ref[...]
        v = v_ref[...].astype(BF16)
        rowpos = qi * tq + lax.broadcasted_iota(I32, (tq, tk), 0)
        colpos = ki * tk + lax.broadcasted_iota(I32, (tq, tk), 1)
        visible = colpos <= rowpos
        for m in range(2):
            ms = slice(m * HEAD_DIM, (m + 1) * HEAD_DIM)
            s = _dot(q[:, ms], k[:, ms], NT_DIMS) * (HEAD_DIM ** -0.5)
            s = jnp.where(visible, s, -jnp.inf)
            m_prev = m_ref[m]
            m_new = jnp.maximum(m_prev, jnp.max(s, axis=1, keepdims=True))
            alpha = jnp.exp(m_prev - m_new)
            p = jnp.exp(s - m_new)
            l_ref[m] = alpha * l_ref[m] + jnp.sum(p, axis=1, keepdims=True)
            acc_ref[m] = alpha * acc_ref[m] + _dot(p.astype(BF16), v)
            m_ref[m] = m_new

    @pl.when(ki == nk - 1)
    def _():
        lam = (jnp.exp(jnp.sum(lq1_ref[...] * lk1_ref[...], axis=1, keepdims=True))
               - jnp.exp(jnp.sum(lq2_ref[...] * lk2_ref[...], axis=1, keepdims=True)) + LAMBDA_INIT)
        o = acc_ref[0] / l_ref[0] - lam * (acc_ref[1] / l_ref[1])
        o_ref[...] = (_rms(o) * nw_ref[...] * (1.0 - LAMBDA_INIT)).astype(o_ref.dtype)


def diff_attention(qr, kr, proj, v_col0, lq1, lk1, lq2, lk2, norm_w, *, batch, seq, tq, tk):
    t = batch * seq
    nq, nk = seq // tq, seq // tk
    vb = v_col0 // DIFF_V_DIM

    def kv_block(qi, ki):
        return jnp.minimum(ki, (qi * tq + tq - 1) // tk)

    vec = pl.BlockSpec((1, HEAD_DIM), lambda b, h, qi, ki: (0, 0))
    return pl.pallas_call(
        functools.partial(_attn_body, tq=tq, tk=tk, nk=nk),
        grid=(batch, DIFF_HEADS, nq, nk),
        in_specs=[pl.BlockSpec((tq, DIFF_V_DIM), lambda b, h, qi, ki: (b * nq + qi, h)),
                  pl.BlockSpec((tk, DIFF_V_DIM), lambda b, h, qi, ki: (b * nk + kv_block(qi, ki), h)),
                  pl.BlockSpec((tk, DIFF_V_DIM), lambda b, h, qi, ki: (b * nk + kv_block(qi, ki), vb + h)),
                  vec, vec, vec, vec,
                  pl.BlockSpec((1, DIFF_V_DIM), lambda b, h, qi, ki: (0, 0))],
        out_specs=pl.BlockSpec((tq, DIFF_V_DIM), lambda b, h, qi, ki: (b * nq + qi, h)),
        out_shape=jax.ShapeDtypeStruct((t, DIFF_WIDTH), BF16),
        scratch_shapes=[pltpu.VMEM((2, tq, 1), F32), pltpu.VMEM((2, tq, 1), F32),
                        pltpu.VMEM((2, tq, DIFF_V_DIM), F32)],
        compiler_params=_params("parallel", "parallel", "parallel", "arbitrary"),
        name="diff_attention",
    )(qr, kr, proj, lq1.reshape(1, -1), lk1.reshape(1, -1), lq2.reshape(1, -1), lk2.reshape(1, -1),
      norm_w.reshape(1, -1))


def _out_proj_body(x_ref, og_ref, od_ref, wg_ref, wd_ref, o_ref):
    o_ref[...] = x_ref[...] + _dot(og_ref[...], wg_ref[...]) + _dot(od_ref[...], wd_ref[...])


def out_proj(x, og, od, w, *, tm, tn):
    t, d = x.shape
    kg, kd = og.shape[1], od.shape[1]
    assert kg == kd
    return pl.pallas_call(
        _out_proj_body,
        grid=(t // tm, d // tn),
        in_specs=[pl.BlockSpec((tm, tn), lambda i, j: (i, j)),
                  pl.BlockSpec((tm, kg), lambda i, j: (i, 0)),
                  pl.BlockSpec((tm, kd), lambda i, j: (i, 0)),
                  pl.BlockSpec((kg, tn), lambda i, j: (0, j)),
                  pl.BlockSpec((kd, tn), lambda i, j: (1, j))],
        out_specs=pl.BlockSpec((tm, tn), lambda i, j: (i, j)),
        out_shape=jax.ShapeDtypeStruct((t, d), F32),
        compiler_params=_params("parallel", "arbitrary"),
        name="out_proj",
    )(x, og, od, w, w)


def _topk_rows(vals, k, payload=None):
    n = vals.shape[0]
    iota = lax.broadcasted_iota(I32, vals.shape, 0).astype(F32)
    out_v, out_i = [], []
    for _ in range(k):
        m = jnp.max(vals, axis=0, keepdims=True)
        idx = jnp.min(jnp.where(vals == m, iota, float(n)), axis=0, keepdims=True)
        hit = iota == idx
        out_v.append(m)
        out_i.append(idx if payload is None else jnp.max(jnp.where(hit, payload, -1.0), axis=0, keepdims=True))
        vals = jnp.where(hit, -jnp.inf, vals)
    return jnp.concatenate(out_v, axis=0), jnp.concatenate(out_i, axis=0)


def _peer_topk_body(q_ref, k1_ref, k2_ref, gate_ref, ia_ref, ib_ref):
    hi = lax.Precision.HIGHEST
    q = q_ref[...]
    s1 = _dot(k1_ref[0], q[:, :HEAD_DIM], NT_DIMS, precision=hi)
    s2 = _dot(k2_ref[0], q[:, HEAD_DIM:], NT_DIMS, precision=hi)
    v1, i1 = _topk_rows(s1, PEER_TOPK)
    v2, i2 = _topk_rows(s2, PEER_TOPK)
    cand = jnp.concatenate([v1[a:a + 1] + v2 for a in range(PEER_TOPK)], axis=0)
    cidx = jnp.concatenate([i1[a:a + 1] * float(N_KEYS) + i2 for a in range(PEER_TOPK)], axis=0)
    top_s, eid = _topk_rows(cand, PEER_TOPK, payload=cidx)
    e = jnp.exp(top_s - top_s[0:1])
    gate_ref[0] = e / jnp.sum(e, axis=0, keepdims=True)
    ia = jnp.floor(eid * (1.0 / N_KEYS))
    ia_ref[0] = ia
    ib_ref[0] = eid - ia * float(N_KEYS)


def peer_topk(q, keys1, keys2, *, tb):
    t = q.shape[0]
    kspec = pl.BlockSpec((1, N_KEYS, HEAD_DIM), lambda i, h: (h, 0, 0))
    ospec = pl.BlockSpec((1, PEER_TOPK, tb), lambda i, h: (h, 0, i))
    oshape = jax.ShapeDtypeStruct((PEER_HEADS, PEER_TOPK, t), F32)
    return pl.pallas_call(
        _peer_topk_body,
        grid=(t // tb, PEER_HEADS),
        in_specs=[pl.BlockSpec((tb, 2 * HEAD_DIM), lambda i, h: (i, h)), kspec, kspec],
        out_specs=[ospec, ospec, ospec],
        out_shape=[oshape, oshape, oshape],
        compiler_params=_params("parallel", "parallel"),
        name="peer_topk",
    )(q, keys1, keys2)


def _peer_mask_body(gate_ref, ia_ref, ib_ref, o_ref, gt_ref, at_ref, bt_ref, gs_ref, *, tg):
    hk = PEER_HEADS * PEER_TOPK
    for src, dst in ((gate_ref, gt_ref), (ia_ref, at_ref), (ib_ref, bt_ref)):
        dst[...] = src[...].reshape(hk, tg).T

    key = lax.broadcasted_iota(I32, (N_KEYS, hk), 0).astype(F32)

    def token(t, carry):
        g = gt_ref[pl.ds(t, 1), :]
        g_hi = g.astype(BF16).astype(F32)
        g_lo = g - g_hi
        hit_a = key == at_ref[pl.ds(t, 1), :]
        hit_b = jnp.where(key == bt_ref[pl.ds(t, 1), :], 1.0, 0.0).astype(BF16)
        a2 = jnp.concatenate([jnp.where(hit_a, g_hi, 0.0).astype(BF16),
                              jnp.where(hit_a, g_lo, 0.0).astype(BF16)], axis=1)
        b2 = jnp.concatenate([hit_b, hit_b], axis=1)
        gs_ref[pl.ds(pl.multiple_of(t * N_KEYS, N_KEYS), N_KEYS), :] = _dot(a2, b2, NT_DIMS)
        return carry

    lax.fori_loop(0, tg, token, 0)
    for c in range(N_KEYS):
        o_ref[c] = gs_ref[pl.ds(c, tg, stride=N_KEYS), :]


def peer_mask(gate, ia, ib, *, tg):
    t = gate.shape[-1]
    ispec = pl.BlockSpec((PEER_HEADS, PEER_TOPK, tg), lambda i: (0, 0, i))
    hk = PEER_HEADS * PEER_TOPK
    return pl.pallas_call(
        functools.partial(_peer_mask_body, tg=tg),
        grid=(t // tg,),
        in_specs=[ispec, ispec, ispec],
        out_specs=pl.BlockSpec((N_KEYS, tg, N_KEYS), lambda i: (0, i, 0)),
        out_shape=jax.ShapeDtypeStruct((N_KEYS, t, N_KEYS), F32),
        scratch_shapes=[pltpu.VMEM((tg, hk), F32)] * 3 + [pltpu.VMEM((tg * N_KEYS, N_KEYS), F32)],
        compiler_params=_params("parallel"),
        name="peer_mask",
    )(gate, ia, ib)


def _peer_expert_body(hf_ref, dn_ref, up_ref, g_ref, o_ref, *, rows_per_step):
    @pl.when(pl.program_id(1) == 0)
    def _():
        o_ref[...] = jnp.zeros_like(o_ref)

    hid = _dot(hf_ref[...], dn_ref[...], NT_DIMS)
    act = 0.5 * hid * (1.0 + lax.erf(hid * (2.0 ** -0.5)))
    g = jnp.concatenate([g_ref[j] for j in range(rows_per_step)], axis=1)
    o_ref[...] += _dot((g * act).astype(BF16), up_ref[...])


def peer_experts(hf, down, up, gmask, *, tb, rows_per_step):
    t, d = hf.shape
    eb = rows_per_step * N_KEYS
    return pl.pallas_call(
        functools.partial(_peer_expert_body, rows_per_step=rows_per_step),
        grid=(t // tb, N_KEYS // rows_per_step),
        in_specs=[pl.BlockSpec((tb, d), lambda i, e: (i, 0)),
                  pl.BlockSpec((eb, d), lambda i, e: (e, 0)),
                  pl.BlockSpec((eb, d), lambda i, e: (e, 0)),
                  pl.BlockSpec((rows_per_step, tb, N_KEYS), lambda i, e: (e, i, 0))],
        out_specs=pl.BlockSpec((tb, d), lambda i, e: (i, 0)),
        out_shape=jax.ShapeDtypeStruct((t, d), F32),
        compiler_params=_params("parallel", "arbitrary"),
        name="peer_experts",
    )(hf, down, up, gmask)


def _final_body(x_ref, y_ref, nw_ref, o_ref):
    o_ref[...] = _rms(x_ref[...] + y_ref[...]) * nw_ref[...]


def residual_norm(x, y, nw, *, tm):
    t, d = x.shape
    blk = pl.BlockSpec((tm, d), lambda i: (i, 0))
    return pl.pallas_call(
        _final_body,
        grid=(t // tm,),
        in_specs=[blk, blk, pl.BlockSpec((1, d), lambda i: (0, 0))],
        out_specs=blk,
        out_shape=jax.ShapeDtypeStruct((t, d), F32),
        compiler_params=_params("parallel"),
        name="residual_norm",
    )(x, y, nw.reshape(1, d))


def _tiles(t, seq):
    return dict(
        tm=min(512, t), tn_in=512, tn_ba=256, tn_out=512, tn_q=512,
        gdn_rows=min(256, seq), rope_rows=min(512, seq), tq=min(512, seq), tk=min(512, seq),
        topk_tb=min(256, t), mask_tg=min(128, t), exp_tb=min(512, t), exp_rows=4, final_tm=min(256, t),
    )


def kernel(x, norm_mix_w, w_in, gdn_conv, gdn_a_log, gdn_dt_bias, gdn_norm_w, lambda_q1, lambda_k1, lambda_q2, lambda_k2, diff_norm_w, w_out, norm_ffn_w, peer_w_query, peer_keys1, peer_keys2, peer_down, peer_up, norm_final_w):
    batch, seq, d = x.shape
    t = batch * seq
    ts = _tiles(t, seq)
    xt = x.reshape(t, d)
    l = 0

    n_main = 4 * GDN_WIDTH
    wi = w_in[l]
    w_main = jnp.concatenate([wi[:, :n_main], wi[:, n_main + 2 * GDN_HEADS:]], axis=1).astype(BF16)
    pad_cols = jnp.zeros((d, HEAD_DIM - GDN_HEADS), F32)
    w_ba = jnp.concatenate([wi[:, n_main:n_main + GDN_HEADS], pad_cols,
                            wi[:, n_main + GDN_HEADS:n_main + 2 * GDN_HEADS], pad_cols], axis=1).astype(BF16)
    proj = norm_matmul(xt, norm_mix_w[l], w_main, tm=ts["tm"], tn=ts["tn_in"])
    ba = norm_matmul(xt, norm_mix_w[l], w_ba, tm=ts["tm"], tn=ts["tn_ba"])

    og = gated_deltanet(proj, ba, gdn_conv[l], gdn_a_log[l], gdn_dt_bias[l], gdn_norm_w[l],
                        batch=batch, seq=seq, rows=ts["gdn_rows"])

    tables = rope_tables(seq)
    qr = rope(proj, n_main, tables, seq=seq, tr=ts["rope_rows"])
    kr = rope(proj, n_main + DIFF_WIDTH, tables, seq=seq, tr=ts["rope_rows"])
    od = diff_attention(qr, kr, proj, n_main + 2 * DIFF_WIDTH, lambda_q1[l], lambda_k1[l], lambda_q2[l],
                        lambda_k2[l], diff_norm_w[l], batch=batch, seq=seq, tq=ts["tq"], tk=ts["tk"])

    x1 = out_proj(xt, og, od, w_out[l].astype(BF16), tm=ts["tm"], tn=ts["tn_out"])

    q, hf = norm_matmul(x1, norm_ffn_w[l], peer_w_query[l].astype(BF16), tm=ts["tm"], tn=ts["tn_q"], emit_h=True)
    gate, ia, ib = peer_topk(q, peer_keys1[l], peer_keys2[l], tb=ts["topk_tb"])
    gmask = peer_mask(gate, ia, ib, tg=ts["mask_tg"])
    y = peer_experts(hf, peer_down[l].astype(BF16), peer_up[l].astype(BF16), gmask,
                     tb=ts["exp_tb"], rows_per_step=ts["exp_rows"])
    out = residual_norm(x1, y, norm_final_w, tm=ts["final_tm"])
    return out.reshape(batch, seq, d)
```
